```python
import math
import jax, jax.numpy as jnp
from jax import lax
import numpy as np

D_MODEL = 4096
BATCH = 4
SEQ = 2048
DEPTH = 2
DEC_BATCH = 8
DEC_SEQ = 4
PAST_LEN = 16384
PAGE_SIZE = 128

EPS = 1e-6
GLA_HEADS = 4
GLA_DK = D_MODEL // (2 * GLA_HEADS)
GLA_DV = D_MODEL // GLA_HEADS
GLA_GATE_RANK = 16
GLA_GATE_TAU = 16.0
GLA_CHUNK = 64
MOBA_HD = 128
MOBA_HEADS = D_MODEL // MOBA_HD
MOBA_BLOCK = 256
MOBA_TOPK = 3
MOBA_QCHUNK = 8
REL_BUCKETS = 32
REL_MAX_DIST = 1024
MLP_HIDDEN = 4 * D_MODEL
NEG_INF = -1e30

kernel_name = "yoco_gla_moba_decoder_step"


def rmsnorm(x, g):
    xf = x.astype(jnp.float32)
    y = xf * lax.rsqrt(jnp.mean(xf * xf, axis=-1, keepdims=True) + EPS)
    return (y * g.astype(jnp.float32)).astype(x.dtype)


def sq_relu_mlp(h, w_up, w_down):
    return jnp.square(jax.nn.relu(h @ w_up)) @ w_down


def t5_bucket(dist):
    n = jnp.maximum(dist, 0)
    max_exact = REL_BUCKETS // 2
    nf = jnp.maximum(n, max_exact).astype(jnp.float32)
    large = max_exact + (jnp.log(nf / max_exact) / math.log(REL_MAX_DIST / max_exact)
                         * (REL_BUCKETS - max_exact)).astype(jnp.int32)
    large = jnp.minimum(large, REL_BUCKETS - 1)
    return jnp.where(n < max_exact, n, large)


def rel_bias(table, dist):
    b = t5_bucket(dist)
    h = jnp.arange(table.shape[1]).reshape((1, -1) + (1,) * (dist.ndim - 2))
    return table[b, h].astype(jnp.float32)


def gla_chunked(q, k, v, gk, s0):
    B, T, H, DK = q.shape
    DV = v.shape[-1]
    C = math.gcd(T, GLA_CHUNK)
    n = T // C

    def to_chunks(a):
        return a.reshape(B, n, C, H, a.shape[-1]).transpose(1, 0, 3, 2, 4)

    mask = jnp.tril(jnp.ones((C, C), dtype=bool))

    def step(S, inp):
        qc, kc, vc, gc = inp
        b = jnp.cumsum(gc, axis=2)
        q_in = qc * jnp.exp(b)
        k_in = kc * jnp.exp(-b)
        att = jnp.where(mask, jnp.einsum('bhtk,bhsk->bhts', q_in, k_in), 0.0)
        o = jnp.einsum('bhtk,bhkv->bhtv', q_in, S) + jnp.einsum('bhts,bhsv->bhtv', att, vc)
        b_last = b[:, :, -1:, :]
        S = (jnp.exp(b_last[:, :, 0, :, None]) * S
             + jnp.einsum('bhsk,bhsv->bhkv', kc * jnp.exp(b_last - b), vc))
        return S, o

    S, o = lax.scan(step, s0, (to_chunks(q), to_chunks(k), to_chunks(v), to_chunks(gk)))
    return o.transpose(1, 0, 3, 2, 4).reshape(B, T, H, DV), S


def gla_mixer(h, w_in, w_gate_up, b_gate, head_norm, w_out, s0):
    B, T, _ = h.shape
    qk = GLA_HEADS * GLA_DK
    vw = GLA_HEADS * GLA_DV
    proj = h @ w_in
    q, k = proj[..., :qk], proj[..., qk:2 * qk]
    v, r = proj[..., 2 * qk:2 * qk + vw], proj[..., 2 * qk + vw:2 * qk + 2 * vw]
    g_low = proj[..., 2 * qk + 2 * vw:]
    gk = jax.nn.log_sigmoid((g_low @ w_gate_up + b_gate).astype(jnp.float32)) / GLA_GATE_TAU

    def heads(a, d):
        return a.reshape(B, T, GLA_HEADS, d).astype(jnp.float32)

    o, s = gla_chunked(heads(q, GLA_DK) * GLA_DK ** -0.5, heads(k, GLA_DK), heads(v, GLA_DV),
                       gk.reshape(B, T, GLA_HEADS, GLA_DK), s0.astype(jnp.float32))
    o = rmsnorm(o, head_norm).reshape(B, T, vw).astype(h.dtype) * jax.nn.silu(r)
    return o @ w_out, s.astype(s0.dtype)


def combine(l_own, v_own, l_sel=None, v_sel=None):
    if l_sel is None:
        p = jax.nn.softmax(l_own, axis=-1).astype(v_own.dtype)
        return jnp.einsum('bhqs,bhsd->bhqd', p, v_own)
    B, H, Q, n, L = l_sel.shape
    lo = l_own.shape[-1]
    p = jax.nn.softmax(jnp.concatenate([l_own, l_sel.reshape(B, H, Q, n * L)], axis=-1),
                       axis=-1).astype(v_own.dtype)
    return (jnp.einsum('bhqs,bhsd->bhqd', p[..., :lo], v_own)
            + jnp.einsum('bhqns,bhqnsd->bhqd', p[..., lo:].reshape(B, H, Q, n, L), v_sel))


def moba_prompt(q, k, v, table):
    B, T, H, HD = q.shape
    nblk = -(-T // MOBA_BLOCK)
    pad = nblk * MOBA_BLOCK - T

    def blocks(a):
        a = jnp.pad(a, ((0, 0), (0, pad), (0, 0), (0, 0)))
        return a.reshape(B, nblk, MOBA_BLOCK, H, HD).transpose(0, 3, 1, 2, 4)

    kb, vb = blocks(k), blocks(v)
    qh = q.transpose(0, 2, 1, 3) * (HD ** -0.5)
    own = jnp.arange(T) // MOBA_BLOCK
    n_sel = min(MOBA_TOPK, nblk - 1)
    qc_len = math.gcd(T, MOBA_QCHUNK)
    blk_pos = jnp.arange(MOBA_BLOCK)
    bi = jnp.arange(B)[:, None, None, None]
    hi = jnp.arange(H)[None, :, None, None]
    if n_sel > 0:
        kmean = jnp.mean(kb, axis=3, dtype=jnp.float32)
        gate = jnp.einsum('bhtd,bhnd->bhtn', qh, kmean, preferred_element_type=jnp.float32)
        gate = jnp.where(jnp.arange(nblk)[None, None, None, :] < own[None, None, :, None], gate, NEG_INF)
        _, sel = lax.top_k(gate, n_sel)

    def attend(c):
        start = c * qc_len
        qc = lax.dynamic_slice_in_dim(qh, start, qc_len, axis=2)
        qpos = start + jnp.arange(qc_len)
        ob = start // MOBA_BLOCK
        k_own = lax.dynamic_index_in_dim(kb, ob, axis=2, keepdims=False)
        v_own = lax.dynamic_index_in_dim(vb, ob, axis=2, keepdims=False)
        dist = (qpos[:, None] - (ob * MOBA_BLOCK + blk_pos)[None, :])[None, None]
        l_own = (jnp.einsum('bhqd,bhsd->bhqs', qc, k_own, preferred_element_type=jnp.float32)
                 + rel_bias(table, dist))
        l_own = jnp.where(dist >= 0, l_own, NEG_INF)
        if n_sel == 0:
            return combine(l_own, v_own)
        sc = lax.dynamic_slice_in_dim(sel, start, qc_len, axis=2)
        k_sel = kb[bi, hi, sc]
        v_sel = vb[bi, hi, sc]
        dist_sel = qpos[None, None, :, None, None] - (sc[..., None] * MOBA_BLOCK + blk_pos)
        l_sel = (jnp.einsum('bhqd,bhqnsd->bhqns', qc, k_sel, preferred_element_type=jnp.float32)
                 + rel_bias(table, dist_sel))
        l_sel = jnp.where((sc < ob)[..., None], l_sel, NEG_INF)
        return combine(l_own, v_own, l_sel, v_sel)

    o = lax.map(attend, jnp.arange(T // qc_len))
    return o.transpose(1, 0, 3, 2, 4).reshape(B, T, H, HD)


def moba_sample(q, k_new, v_new, cache_k, cache_v, page_table, table):
    B, S, H, HD = q.shape
    n_pages = page_table.shape[1]
    past = n_pages * PAGE_SIZE
    ppb = MOBA_BLOCK // PAGE_SIZE
    nfp = past // MOBA_BLOCK
    qh = q.transpose(0, 2, 1, 3) * (HD ** -0.5)
    qpos = past + jnp.arange(S)
    tail = page_table[:, nfp * ppb:]
    n_tail_rows = tail.shape[1] * PAGE_SIZE

    def own_rows(cache, new):
        rows = cache[tail].reshape(B, n_tail_rows, H, HD)
        return jnp.concatenate([rows, new], axis=1).transpose(0, 2, 1, 3)

    k_own, v_own = own_rows(cache_k, k_new), own_rows(cache_v, v_new)
    dist = (qpos[:, None] - (nfp * MOBA_BLOCK + jnp.arange(n_tail_rows + S))[None, :])[None, None]
    l_own = (jnp.einsum('bhqd,bhsd->bhqs', qh, k_own, preferred_element_type=jnp.float32)
             + rel_bias(table, dist))
    l_own = jnp.where(dist >= 0, l_own, NEG_INF)
    n_sel = min(MOBA_TOPK, nfp)
    if n_sel == 0:
        return combine(l_own, v_own).transpose(0, 2, 1, 3)
    page_sum = jnp.sum(cache_k, axis=1, dtype=jnp.float32)
    kmean = (page_sum[page_table[:, :nfp * ppb]].reshape(B, nfp, ppb, H, HD).sum(axis=2)
             .transpose(0, 2, 1, 3) / MOBA_BLOCK)
    gate = jnp.einsum('bhqd,bhnd->bhqn', qh, kmean, preferred_element_type=jnp.float32)
    _, sel = lax.top_k(gate, n_sel)
    bi = jnp.arange(B)[:, None, None, None, None]
    phys = page_table[bi, sel[..., None] * ppb + jnp.arange(ppb)]
    hi = jnp.arange(H)[None, :, None, None, None, None]

    def gather(cache):
        return cache[phys[..., None], jnp.arange(PAGE_SIZE), hi].reshape(B, H, S, n_sel, MOBA_BLOCK, HD)

    k_sel, v_sel = gather(cache_k), gather(cache_v)
    dist_sel = qpos[None, None, :, None, None] - (sel[..., None] * MOBA_BLOCK + jnp.arange(MOBA_BLOCK))
    l_sel = (jnp.einsum('bhqd,bhqnsd->bhqns', qh, k_sel, preferred_element_type=jnp.float32)
             + rel_bias(table, dist_sel))
    return combine(l_own, v_own, l_sel, v_sel).transpose(0, 2, 1, 3)


def setup_inputs(seed: int = 0) -> dict:
    key = jax.random.key(seed)
    ks = jax.random.split(key, 24)
    n_a = DEPTH // 2
    n_b = DEPTH - n_a
    n_pages = PAST_LEN // PAGE_SIZE
    n_used = DEC_BATCH * n_pages
    n_pool = n_used + max(1, n_used // 4)
    gla_in_width = 2 * GLA_HEADS * GLA_DK + 2 * GLA_HEADS * GLA_DV + GLA_GATE_RANK
    kvw = MOBA_HEADS * MOBA_HD

    def nrm(k, shape, scale):
        return scale * jax.random.normal(k, shape, jnp.float32)

    page_table = jax.random.permutation(ks[5], n_pool)[:n_used].reshape(DEC_BATCH, n_pages).astype(jnp.int32)
    return {
        "x_prompt": nrm(ks[0], (BATCH, SEQ, D_MODEL), 1.0),
        "x_sample": nrm(ks[1], (DEC_BATCH, DEC_SEQ, D_MODEL), 1.0),
        "state_gla": nrm(ks[2], (n_a, DEC_BATCH, GLA_HEADS, GLA_DK, GLA_DV), 1.0),
        "cache_k": nrm(ks[3], (n_pool, PAGE_SIZE, MOBA_HEADS, MOBA_HD), 1.0),
        "cache_v": nrm(ks[4], (n_pool, PAGE_SIZE, MOBA_HEADS, MOBA_HD), 1.0),
        "page_table": page_table,
        "rel_bias_table": nrm(ks[6], (REL_BUCKETS, MOBA_HEADS), 0.5),
        "norm_mix": 1.0 + nrm(ks[7], (DEPTH, D_MODEL), 0.05),
        "norm_mlp": 1.0 + nrm(ks[8], (DEPTH, D_MODEL), 0.05),
        "norm_kv": 1.0 + nrm(ks[9], (D_MODEL,), 0.05),
        "norm_final": 1.0 + nrm(ks[10], (D_MODEL,), 0.05),
        "w_gla_in": nrm(ks[11], (n_a, D_MODEL, gla_in_width), D_MODEL ** -0.5),
        "w_gla_gate_up": nrm(ks[12], (n_a, GLA_GATE_RANK, GLA_HEADS * GLA_DK), GLA_GATE_RANK ** -0.5),
        "b_gla_gate": nrm(ks[13], (n_a, GLA_HEADS * GLA_DK), 0.1),
        "gla_head_norm": 1.0 + nrm(ks[14], (n_a, GLA_DV), 0.05),
        "w_gla_out": nrm(ks[15], (n_a, GLA_HEADS * GLA_DV, D_MODEL), (GLA_HEADS * GLA_DV) ** -0.5),
        "w_kv": nrm(ks[16], (D_MODEL, 2 * kvw), D_MODEL ** -0.5),
        "w_q_moba": nrm(ks[17], (n_b, D_MODEL, kvw), D_MODEL ** -0.5),
        "w_o_moba": nrm(ks[18], (n_b, kvw, D_MODEL), kvw ** -0.5),
        "w_mlp_up": nrm(ks[19], (DEPTH, D_MODEL, MLP_HIDDEN), D_MODEL ** -0.5),
        "w_mlp_down": nrm(ks[20], (DEPTH, MLP_HIDDEN, D_MODEL), MLP_HIDDEN ** -0.5),
    }


def reference(x_prompt, x_sample, state_gla, cache_k, cache_v, page_table, rel_bias_table,
              norm_mix, norm_mlp, norm_kv, norm_final, w_gla_in, w_gla_gate_up, b_gla_gate,
              gla_head_norm, w_gla_out, w_kv, w_q_moba, w_o_moba, w_mlp_up, w_mlp_down):
    n_a = w_gla_in.shape[0]
    n_layers = norm_mix.shape[0]
    kvw = MOBA_HEADS * MOBA_HD

    def trunk(x, gla_s0, moba_fn):
        B, T, _ = x.shape
        states = []
        k_sh = v_sh = None
        for l in range(n_layers):
            h = rmsnorm(x, norm_mix[l])
            if l < n_a:
                o, s = gla_mixer(h, w_gla_in[l], w_gla_gate_up[l], b_gla_gate[l],
                                 gla_head_norm[l], w_gla_out[l], gla_s0[l])
                states.append(s)
            else:
                if l == n_a:
                    kv = rmsnorm(x, norm_kv) @ w_kv
                    k_sh = kv[..., :kvw].reshape(B, T, MOBA_HEADS, MOBA_HD)
                    v_sh = kv[..., kvw:].reshape(B, T, MOBA_HEADS, MOBA_HD)
                q = (h @ w_q_moba[l - n_a]).reshape(B, T, MOBA_HEADS, MOBA_HD)
                o = moba_fn(q, k_sh, v_sh).reshape(B, T, kvw) @ w_o_moba[l - n_a]
            x = x + o
            x = x + sq_relu_mlp(rmsnorm(x, norm_mlp[l]), w_mlp_up[l], w_mlp_down[l])
        return rmsnorm(x, norm_final), jnp.stack(states), k_sh, v_sh

    s0_prompt = jnp.zeros((n_a, x_prompt.shape[0], GLA_HEADS, GLA_DK, GLA_DV), x_prompt.dtype)
    y_prompt, s_prompt, k_prompt, v_prompt = trunk(
        x_prompt, s0_prompt, lambda q, k, v: moba_prompt(q, k, v, rel_bias_table))
    y_sample, s_sample, k_sample, v_sample = trunk(
        x_sample, state_gla,
        lambda q, k, v: moba_sample(q, k, v, cache_k, cache_v, page_table, rel_bias_table))
    return (y_prompt, y_sample, s_prompt, s_sample, k_prompt, v_prompt, k_sample, v_sample)
```

```python
import functools
import math

import jax
import jax.numpy as jnp
from jax import lax
from jax.experimental import pallas as pl
from jax.experimental.pallas import tpu as pltpu

F32 = jnp.float32
BF16 = jnp.bfloat16

EPS = 1e-6
GLA_HEADS = 4
GLA_GATE_RANK = 16
GLA_GATE_TAU = 16.0
GLA_CHUNK = 64
MOBA_HD = 128
MOBA_BLOCK = 256
MOBA_TOPK = 3
PAGE_SIZE = 128
REL_BUCKETS = 32
REL_MAX_DIST = 1024
NEG_INF = -1e30

V7X_LANES = 128
V7X_BF16_SUBLANES = 16
V7X_VMEM_BYTES = 64 * 1024 * 1024
V7X_VMEM_LIMIT_BYTES = V7X_VMEM_BYTES - 8 * 1024 * 1024
MM_VMEM_BUDGET_BYTES = 40 * 1024 * 1024


def _cparams(semantics):
    return pltpu.CompilerParams(dimension_semantics=semantics, vmem_limit_bytes=V7X_VMEM_LIMIT_BYTES)


def _dot_nt(a, b):
    return lax.dot_general(a, b, (((1,), (1,)), ((), ())), preferred_element_type=F32)


def _dot_tn(a, b):
    return lax.dot_general(a, b, (((0,), (0,)), ((), ())), preferred_element_type=F32)


def _dot(a, b):
    return jnp.dot(a, b, preferred_element_type=F32)


def _div_pow2(x, d):
    assert d & (d - 1) == 0
    return lax.shift_right_logical(x, d.bit_length() - 1)


def _t5_bucket(n):
    max_exact = REL_BUCKETS // 2
    if n < max_exact:
        return n
    large = max_exact + int(math.log(n / max_exact) / math.log(REL_MAX_DIST / max_exact) * (REL_BUCKETS - max_exact))
    return min(large, REL_BUCKETS - 1)


def _t5_thresholds():
    thr, n = [], 0
    for m in range(REL_BUCKETS):
        while _t5_bucket(n) < m:
            n += 1
        thr.append(n)
    return thr


def _bias_lookup(dist, tab, lo=None, hi=None):
    thr = _t5_thresholds()
    lo = 0 if lo is None else max(lo, 0)
    base = _t5_bucket(lo)
    val = jnp.full(dist.shape, tab(base), F32)
    for m in range(base + 1, REL_BUCKETS):
        if hi is not None and thr[m] > hi:
            break
        val = jnp.where(dist >= thr[m], tab(m), val)
    return val


def _mm_body(*refs, nk, epi, scale):
    if epi == "res":
        x_ref, w_ref, r_ref, o_ref, *scratch = refs
    else:
        x_ref, w_ref, o_ref, *scratch = refs

    def finish(acc):
        if epi == "res":
            o_ref[...] = r_ref[...] + acc
        elif epi == "relu2":
            a = jnp.maximum(acc, 0.0)
            o_ref[...] = (a * a).astype(o_ref.dtype)
        elif epi == "scale":
            o_ref[...] = (acc * scale).astype(o_ref.dtype)
        else:
            o_ref[...] = acc.astype(o_ref.dtype)

    part = _dot(x_ref[...], w_ref[...])
    if nk == 1:
        finish(part)
    else:
        acc_ref = scratch[0]
        k = pl.program_id(2)

        @pl.when(k == 0)
        def _():
            acc_ref[...] = part

        @pl.when(k > 0)
        def _():
            acc_ref[...] += part

        @pl.when(k == nk - 1)
        def _():
            finish(acc_ref[...])


def _mm_tiles(m, k, n, out_bytes, has_res):
    tm = min(m, 1024)
    tk = k if k <= 4096 else 2048
    for tn in (1024, 512, 256, 128):
        if n % tn:
            continue
        windows = tm * tk * 2 + tk * tn * 2 + tm * tn * out_bytes + (tm * tn * 4 if has_res else 0)
        acc = tm * tn * 4 if tk != k else 0
        if 2 * windows + acc <= MM_VMEM_BUDGET_BYTES:
            return tm, tn, tk
    raise ValueError(f"no matmul tiling fits VMEM for {(m, k, n)}")


def _matmul(x, w, *, n0=0, n=None, epi="plain", out_dtype=F32, res=None, scale=1.0, name="mm"):
    m, k = x.shape
    n = w.shape[1] if n is None else n
    assert x.dtype == BF16 and w.dtype == BF16 and w.shape[0] == k
    assert (res is not None) == (epi == "res")
    tm, tn, tk = _mm_tiles(m, k, n, jnp.dtype(out_dtype).itemsize, res is not None)
    assert m % tm == 0 and n % tn == 0 and k % tk == 0 and n0 % tn == 0
    nk = k // tk
    j0 = n0 // tn
    in_specs = [
        pl.BlockSpec((tm, tk), lambda i, j, kk: (i, kk)),
        pl.BlockSpec((tk, tn), lambda i, j, kk: (kk, j + j0)),
    ]
    args = [x, w]
    if res is not None:
        in_specs.append(pl.BlockSpec((tm, tn), lambda i, j, kk: (i, j)))
        args.append(res)
    return pl.pallas_call(
        functools.partial(_mm_body, nk=nk, epi=epi, scale=scale),
        out_shape=jax.ShapeDtypeStruct((m, n), out_dtype),
        grid=(m // tm, n // tn, nk),
        in_specs=in_specs,
        out_specs=pl.BlockSpec((tm, tn), lambda i, j, kk: (i, j)),
        scratch_shapes=[pltpu.VMEM((tm, tn), F32)] if nk > 1 else [],
        compiler_params=_cparams(("parallel", "parallel", "arbitrary")),
        name=name,
    )(*args)


def _rms_body(x_ref, g_ref, *o_refs):
    x = x_ref[...]
    y = x * lax.rsqrt(jnp.mean(x * x, axis=-1, keepdims=True) + EPS)
    for gi, o_ref in enumerate(o_refs):
        o_ref[...] = (y * g_ref[gi:gi + 1, :]).astype(o_ref.dtype)


def _rmsnorm(x, gains, out_dtype, name="rmsnorm"):
    m, d = x.shape
    g = gains.shape[0]
    tm = min(m, 256)
    assert m % tm == 0
    outs = pl.pallas_call(
        _rms_body,
        out_shape=[jax.ShapeDtypeStruct((m, d), out_dtype)] * g,
        grid=(m // tm,),
        in_specs=[pl.BlockSpec((tm, d), lambda i: (i, 0)), pl.BlockSpec((g, d), lambda i: (0, 0))],
        out_specs=[pl.BlockSpec((tm, d), lambda i: (i, 0))] * g,
        compiler_params=_cparams(("parallel",)),
        name=name,
    )(x, gains)
    return outs


def _gate_body(h_ref, wl_ref, wu_ref, b_ref, o_ref):
    g_low = _dot(h_ref[...], wl_ref[...])
    z = _dot(g_low.astype(BF16), wu_ref[...]) + b_ref[...]
    log_sig = jnp.minimum(z, 0.0) - jnp.log1p(jnp.exp(-jnp.abs(z)))
    o_ref[...] = log_sig / GLA_GATE_TAU


def _gla_gate(h, w_low, w_up, b):
    m, d = h.shape
    qk = w_up.shape[1]
    tm = min(m, 512)
    assert m % tm == 0
    return pl.pallas_call(
        _gate_body,
        out_shape=jax.ShapeDtypeStruct((m, qk), F32),
        grid=(m // tm,),
        in_specs=[
            pl.BlockSpec((tm, d), lambda i: (i, 0)),
            pl.BlockSpec(w_low.shape, lambda i: (0, 0)),
            pl.BlockSpec(w_up.shape, lambda i: (0, 0)),
            pl.BlockSpec((1, qk), lambda i: (0, 0)),
        ],
        out_specs=pl.BlockSpec((tm, qk), lambda i: (i, 0)),
        compiler_params=_cparams(("parallel",)),
        name="gla_gate",
    )(h, w_low, w_up, b)


def _gla_body(*refs, chunk, dk, dv, has_s0):
    if has_s0:
        q_ref, k_ref, v_ref, r_ref, g_ref, hn_ref, s0_ref, o_ref, so_ref, s_ref = refs
    else:
        q_ref, k_ref, v_ref, r_ref, g_ref, hn_ref, o_ref, so_ref, s_ref = refs
    c = pl.program_id(2)

    @pl.when(c == 0)
    def _():
        s_ref[...] = s0_ref[...] if has_s0 else jnp.zeros_like(s_ref)

    q = q_ref[...] * dk ** -0.5
    k = k_ref[...]
    v = v_ref[...].astype(BF16)
    b = g_ref[...]
    rows = lax.broadcasted_iota(jnp.int32, b.shape, 0)
    shift = 1
    while shift < chunk:
        b = b + jnp.where(rows >= shift, pltpu.roll(b, shift, 0), 0.0)
        shift *= 2
    q_in = (q * jnp.exp(b)).astype(BF16)
    k_in = (k * jnp.exp(-b)).astype(BF16)
    att = _dot_nt(q_in, k_in)
    tri = lax.broadcasted_iota(jnp.int32, att.shape, 0) >= lax.broadcasted_iota(jnp.int32, att.shape, 1)
    att = jnp.where(tri, att, 0.0)
    s = s_ref[...]
    o = _dot(q_in, s.astype(BF16)) + _dot(att.astype(BF16), v)
    b_last = b[chunk - 1:chunk, :]
    k_dec = (k * jnp.exp(b_last - b)).astype(BF16)
    decay_rows = jnp.transpose(jnp.broadcast_to(jnp.exp(b_last), (V7X_LANES, dk)))
    decay = jnp.concatenate([decay_rows] * (dv // V7X_LANES), axis=1)
    s_new = decay * s + _dot_tn(k_dec, v)
    s_ref[...] = s_new

    y = o * lax.rsqrt(jnp.mean(o * o, axis=-1, keepdims=True) + EPS) * hn_ref[...]
    r = r_ref[...]
    o_ref[...] = (y * (r * jax.nn.sigmoid(r))).astype(o_ref.dtype)

    @pl.when(c == pl.num_programs(2) - 1)
    def _():
        so_ref[...] = s_new


def _gla(proj, gk, head_norm, s0, *, batch, seq, chunk, out_dtype):
    h = GLA_HEADS
    dk = gk.shape[1] // h
    dv = (proj.shape[1] - 2 * h * dk) // (2 * h)
    assert seq % chunk == 0 and (h * dk) % dv == 0
    nc = seq // chunk
    v_blk0 = (2 * h * dk) // dv
    r_blk0 = v_blk0 + h
    row = lambda b, hh, c: b * nc + c
    in_specs = [
        pl.BlockSpec((chunk, dk), lambda b, hh, c: (row(b, hh, c), hh)),
        pl.BlockSpec((chunk, dk), lambda b, hh, c: (row(b, hh, c), h + hh)),
        pl.BlockSpec((chunk, dv), lambda b, hh, c: (row(b, hh, c), v_blk0 + hh)),
        pl.BlockSpec((chunk, dv), lambda b, hh, c: (row(b, hh, c), r_blk0 + hh)),
        pl.BlockSpec((chunk, dk), lambda b, hh, c: (row(b, hh, c), hh)),
        pl.BlockSpec((1, dv), lambda b, hh, c: (0, 0)),
    ]
    args = [proj, proj, proj, proj, gk, head_norm]
    if s0 is not None:
        in_specs.append(pl.BlockSpec((None, None, dk, dv), lambda b, hh, c: (b, hh, 0, 0)))
        args.append(s0)
    return pl.pallas_call(
        functools.partial(_gla_body, chunk=chunk, dk=dk, dv=dv, has_s0=s0 is not None),
        out_shape=[
            jax.ShapeDtypeStruct((batch * seq, h * dv), out_dtype),
            jax.ShapeDtypeStruct((batch, h, dk, dv), F32),
        ],
        grid=(batch, h, nc),
        in_specs=in_specs,
        out_specs=[
            pl.BlockSpec((chunk, dv), lambda b, hh, c: (row(b, hh, c), hh)),
            pl.BlockSpec((None, None, dk, dv), lambda b, hh, c: (b, hh, 0, 0)),
        ],
        scratch_shapes=[pltpu.VMEM((dk, dv), F32)],
        compiler_params=_cparams(("parallel", "parallel", "arbitrary")),
        name="gla_recurrence",
    )(*args)


def _moba_prompt_body(tab_ref, q_ref, k_ref, v_ref, o_ref, kb_ref, vt_ref, bias_ref, sel_ref, *, nblk):
    blk = MOBA_BLOCK
    hh = pl.program_id(0)
    b = pl.program_id(1)
    tab = lambda m: tab_ref[m, hh]

    @pl.when(b == 0)
    def _():
        ki = lax.broadcasted_iota(jnp.int32, (blk, blk), 0)
        qj = lax.broadcasted_iota(jnp.int32, (blk, blk), 1)
        for d in range(nblk):
            dist = d * blk + qj - ki
            bias_ref[d] = _bias_lookup(dist, tab, lo=d * blk - (blk - 1), hi=d * blk + (blk - 1))

    k = k_ref[...]
    t = k.shape[0]
    kb_ref[...] = k.astype(BF16)
    v = v_ref[...]
    for n in range(nblk):
        vt_ref[n] = jnp.transpose(v[n * blk:(n + 1) * blk, :]).astype(BF16)

    kmean = jnp.sum(k.reshape(nblk, blk, k.shape[1]), axis=1) * (1.0 / blk)
    q_all = q_ref[...]
    gate = _dot_nt(kmean.astype(BF16), q_all)
    n_iota = lax.broadcasted_iota(jnp.int32, (nblk, t), 0)
    own = _div_pow2(lax.broadcasted_iota(jnp.int32, (nblk, t), 1), blk)
    gate = jnp.where(n_iota < own, gate, NEG_INF)
    rank = jnp.zeros((nblk, t), F32)
    for m in range(nblk):
        gm = jnp.broadcast_to(gate[m:m + 1, :], (nblk, t))
        tie_before = jnp.where(n_iota > m, 1.0, 0.0)
        rank = rank + jnp.where(gm > gate, 1.0, jnp.where(gm == gate, tie_before, 0.0))
    sel = jnp.where(rank < MOBA_TOPK, jnp.where(n_iota < own, 1.0, 0.0), 0.0)
    for qi in range(nblk):
        sel_ref[qi] = sel[:, qi * blk:(qi + 1) * blk]

    ki = lax.broadcasted_iota(jnp.int32, (blk, blk), 0)
    qj = lax.broadcasted_iota(jnp.int32, (blk, blk), 1)
    causal = ki <= qj

    def q_tile(qi, carry):
        row0 = pl.multiple_of(qi * blk, blk)
        q_t = q_ref[pl.ds(row0, blk), :]
        s = _dot_nt(kb_ref[pl.ds(row0, blk), :], q_t) + bias_ref[0]
        s = jnp.where(causal, s, NEG_INF)
        m = jnp.max(s, axis=0, keepdims=True)
        p = jnp.exp(s - m)
        l = jnp.sum(p, axis=0, keepdims=True)
        acc = _dot(vt_ref[qi], p.astype(BF16))

        def past_block(j, mla):
            m, l, acc = mla
            k0 = pl.multiple_of(j * blk, blk)
            s = _dot_nt(kb_ref[pl.ds(k0, blk), :], q_t) + bias_ref[qi - j]
            s = jnp.where(sel_ref[qi, pl.ds(j, 1), :] > 0.0, s, NEG_INF)
            m_new = jnp.maximum(m, jnp.max(s, axis=0, keepdims=True))
            alpha = jnp.exp(m - m_new)
            p = jnp.exp(s - m_new)
            l = alpha * l + jnp.sum(p, axis=0, keepdims=True)
            acc = alpha * acc + _dot(vt_ref[j], p.astype(BF16))
            return m_new, l, acc

        m, l, acc = lax.fori_loop(0, qi, past_block, (m, l, acc))
        o_ref[pl.ds(row0, blk), :] = jnp.transpose(acc / l).astype(o_ref.dtype)
        return carry

    lax.fori_loop(0, nblk, q_tile, 0)


def _moba_prompt(q, k, v, table, *, batch, seq):
    hd = MOBA_HD
    nh = q.shape[1] // hd
    assert seq % MOBA_BLOCK == 0
    nblk = seq // MOBA_BLOCK
    spec = pl.BlockSpec((seq, hd), lambda hh, b: (b, hh))
    return pl.pallas_call(
        functools.partial(_moba_prompt_body, nblk=nblk),
        out_shape=jax.ShapeDtypeStruct(q.shape, BF16),
        grid=(nh, batch),
        in_specs=[pl.BlockSpec(memory_space=pltpu.SMEM), spec, spec, spec],
        out_specs=spec,
        scratch_shapes=[
            pltpu.VMEM((seq, hd), BF16),
            pltpu.VMEM((nblk, hd, MOBA_BLOCK), BF16),
            pltpu.VMEM((nblk, MOBA_BLOCK, MOBA_BLOCK), F32),
            pltpu.VMEM((nblk, nblk, MOBA_BLOCK), F32),
        ],
        compiler_params=_cparams(("arbitrary", "arbitrary")),
        name="moba_prompt",
    )(table, q, k, v)


def _moba_select_body(pt_ref, ka_ref, kb_ref, q_ref, o_ref, kmt_ref, *, nfp, nh, nq):
    del pt_ref
    n = pl.program_id(1)
    hd = MOBA_HD
    ps = (jnp.sum(ka_ref[...], axis=0) + jnp.sum(kb_ref[...], axis=0)) * (1.0 / MOBA_BLOCK)
    kmt_ref[pl.ds(pl.multiple_of(n * nh, nh), nh), :] = ps

    @pl.when(n == nfp - 1)
    def _():
        q = q_ref[...]
        out_lane = lax.broadcasted_iota(jnp.int32, (q.shape[0], V7X_LANES), 1)
        for hh in range(nh):
            km = kmt_ref[pl.ds(hh, nfp, stride=nh), :]
            gate = _dot_nt(q[:, hh * hd:(hh + 1) * hd], km.astype(BF16))
            lane = lax.broadcasted_iota(jnp.int32, gate.shape, 1).astype(F32)
            res = jnp.zeros(out_lane.shape, F32)
            for t in range(MOBA_TOPK):
                m = jnp.max(gate, axis=1, keepdims=True)
                pick = jnp.min(jnp.where(gate == m, lane, float(nfp)), axis=1, keepdims=True)
                res = jnp.where(out_lane == t, pick, res)
                gate = jnp.where(lane == pick, -jnp.inf, gate)
            o_ref[hh * nq:(hh + 1) * nq, :] = res[:nq].astype(jnp.int32)


def _moba_select(q, cache_k, page_table, *, nfp, nq):
    bsz, rows, width = q.shape
    _, page, nh, hd = cache_k.shape
    ppb = MOBA_BLOCK // page
    assert ppb == 2 and page == PAGE_SIZE
    page_spec = lambda off: pl.BlockSpec((None, page, nh, hd), lambda b, n, pt: (pt[b, ppb * n + off], 0, 0, 0))
    return pl.pallas_call(
        functools.partial(_moba_select_body, nfp=nfp, nh=nh, nq=nq),
        out_shape=jax.ShapeDtypeStruct((bsz, nh * nq, V7X_LANES), jnp.int32),
        grid_spec=pltpu.PrefetchScalarGridSpec(
            num_scalar_prefetch=1,
            grid=(bsz, nfp),
            in_specs=[page_spec(0), page_spec(1), pl.BlockSpec((None, rows, width), lambda b, n, pt: (b, 0, 0))],
            out_specs=pl.BlockSpec((None, nh * nq, V7X_LANES), lambda b, n, pt: (b, 0, 0)),
            scratch_shapes=[pltpu.VMEM((nfp * nh, hd), F32)],
        ),
        compiler_params=_cparams(("parallel", "arbitrary")),
        name="moba_select",
    )(page_table, cache_k, cache_k, q)


def _moba_sample_body(sel_ref, pt_ref, tab_ref, q_ref, kn_ref, vn_ref, ck_ref, cv_ref, o_ref,
                      kbuf, vbuf, knew, vnew, ksem, vsem, *, nh, nq, n_pages, past):
    t = pl.program_id(0)
    nsteps = pl.num_programs(0)
    blk = MOBA_BLOCK
    ppb = blk // PAGE_SIZE
    nsel = nq * MOBA_TOPK
    slot = t % 2

    def copies(step, slot_):
        bb = step // nh
        hh = step % nh
        out = []
        for c in range(nsel):
            block_id = sel_ref[step * nsel + c]
            for pg in range(ppb):
                page = pt_ref[bb * n_pages + block_id * ppb + pg]
                dst = pl.ds((c * ppb + pg) * PAGE_SIZE, PAGE_SIZE)
                out.append(pltpu.make_async_copy(ck_ref.at[page, :, hh, :], kbuf.at[slot_, dst, :], ksem.at[slot_]))
                out.append(pltpu.make_async_copy(cv_ref.at[page, :, hh, :], vbuf.at[slot_, dst, :], vsem.at[slot_]))
        return out

    @pl.when(t == 0)
    def _():
        for cp in copies(t, slot):
            cp.start()

    @pl.when(t + 1 < nsteps)
    def _():
        for cp in copies(t + 1, 1 - slot):
            cp.start()

    hh = t % nh
    tab = lambda m: tab_ref[m, hh]
    q = q_ref[...]
    rows = q.shape[0]
    width = nsel * blk

    knew[...] = jnp.zeros_like(knew)
    vnew[...] = jnp.zeros_like(vnew)
    knew[pl.ds(0, rows), :] = kn_ref[...]
    vnew[pl.ds(0, rows), :] = vn_ref[...]
    lo = _dot_nt(q, knew[...].astype(BF16))
    r_o = lax.broadcasted_iota(jnp.int32, lo.shape, 0)
    t_o = lax.broadcasted_iota(jnp.int32, lo.shape, 1)
    d_o = r_o - t_o
    lo = lo + _bias_lookup(d_o, tab, lo=0, hi=nq - 1)
    lo = jnp.where(d_o >= 0, jnp.where(t_o < nq, lo, NEG_INF), NEG_INF)

    lane = lax.broadcasted_iota(jnp.int32, (rows, width), 1)
    row = lax.broadcasted_iota(jnp.int32, (rows, width), 0)
    kstart = jnp.zeros((rows, width), jnp.int32)
    for c in range(nsel):
        kstart = jnp.where(_div_pow2(lane, blk) == c, sel_ref[t * nsel + c] * blk, kstart)
    dist = past + row - (kstart + (lane & (blk - 1)))

    for cp in copies(t, slot):
        cp.wait()

    ls = _dot_nt(q, kbuf[slot].astype(BF16)) + _bias_lookup(dist, tab)
    first = row * (MOBA_TOPK * blk)
    ls = jnp.where(lane >= first, jnp.where(lane < first + MOBA_TOPK * blk, ls, NEG_INF), NEG_INF)

    m = jnp.maximum(jnp.max(lo, axis=1, keepdims=True), jnp.max(ls, axis=1, keepdims=True))
    p_o = jnp.exp(lo - m)
    p_s = jnp.exp(ls - m)
    den = jnp.sum(p_o, axis=1, keepdims=True) + jnp.sum(p_s, axis=1, keepdims=True)
    out = _dot(p_o.astype(BF16), vnew[...].astype(BF16)) + _dot(p_s.astype(BF16), vbuf[slot].astype(BF16))
    o_ref[...] = out / den


def _moba_sample(q, k_new, v_new, cache_k, cache_v, sel, page_table, table, *, nq, past):
    bsz, rows, width = q.shape
    hd = MOBA_HD
    nh = width // hd
    n_pages = page_table.shape[0] // bsz
    nsel = nq * MOBA_TOPK
    tok = pl.BlockSpec((None, rows, hd), lambda t, sel, pt: (t // nh, 0, t % nh))
    return pl.pallas_call(
        functools.partial(_moba_sample_body, nh=nh, nq=nq, n_pages=n_pages, past=past),
        out_shape=jax.ShapeDtypeStruct((bsz, rows, width), F32),
        grid_spec=pltpu.PrefetchScalarGridSpec(
            num_scalar_prefetch=2,
            grid=(bsz * nh,),
            in_specs=[
                pl.BlockSpec(memory_space=pltpu.SMEM),
                tok, tok, tok,
                pl.BlockSpec(memory_space=pl.ANY),
                pl.BlockSpec(memory_space=pl.ANY),
            ],
            out_specs=tok,
            scratch_shapes=[
                pltpu.VMEM((2, nsel * MOBA_BLOCK, hd), F32),
                pltpu.VMEM((2, nsel * MOBA_BLOCK, hd), F32),
                pltpu.VMEM((V7X_LANES, hd), F32),
                pltpu.VMEM((V7X_LANES, hd), F32),
                pltpu.SemaphoreType.DMA((2,)),
                pltpu.SemaphoreType.DMA((2,)),
            ],
        ),
        compiler_params=_cparams(("arbitrary",)),
        name="moba_sample",
    )(sel, page_table, table, q, k_new, v_new, cache_k, cache_v)


def _mlp(x, norm_g, w_up, w_down, tag):
    (h,) = _rmsnorm(x, norm_g[None, :], BF16, name=f"rms_mlp_{tag}")
    u = _matmul(h, w_up, epi="relu2", out_dtype=BF16, name=f"mlp_up_{tag}")
    return _matmul(u, w_down, epi="res", res=x, name=f"mlp_down_{tag}")


def _pad_rows(a, batch, seq, rows):
    w = a.shape[1]
    return jnp.pad(a.reshape(batch, seq, w), ((0, 0), (0, rows - seq), (0, 0))).reshape(batch * rows, w)


def _trunk(x, s0, moba_fn, wts, *, batch, seq, tag):
    (norm_mix, norm_mlp, norm_kv, norm_final, w_in, w_low, w_gate_up, b_gate, head_norm, w_out,
     w_kv, w_q, w_o, w_up, w_down) = wts
    kvw = w_q.shape[1]
    qkvr = w_in.shape[1] - GLA_GATE_RANK

    (h0,) = _rmsnorm(x, norm_mix[0:1], BF16, name=f"rms_mix0_{tag}")
    proj = _matmul(h0, w_in, n=qkvr, name=f"gla_in_{tag}")
    gk = _gla_gate(h0, w_low, w_gate_up, b_gate)
    if seq % GLA_CHUNK == 0:
        og, s_new = _gla(proj, gk, head_norm, s0, batch=batch, seq=seq, chunk=GLA_CHUNK, out_dtype=BF16)
    else:
        rows = V7X_BF16_SUBLANES
        assert seq <= rows
        og, s_new = _gla(_pad_rows(proj, batch, seq, rows), _pad_rows(gk, batch, seq, rows), head_norm, s0,
                         batch=batch, seq=rows, chunk=rows, out_dtype=BF16)
        og = og.reshape(batch, rows, -1)[:, :seq].reshape(batch * seq, -1)
    x = _matmul(og, w_out, epi="res", res=x, name=f"gla_out_{tag}")
    x = _mlp(x, norm_mlp[0], w_up[0], w_down[0], f"0_{tag}")

    h1, kvn = _rmsnorm(x, jnp.stack([norm_mix[1], norm_kv]), BF16, name=f"rms_mix1_{tag}")
    k_sh = _matmul(kvn, w_kv, n0=0, n=kvw, name=f"k_proj_{tag}")
    v_sh = _matmul(kvn, w_kv, n0=kvw, n=kvw, name=f"v_proj_{tag}")
    q = _matmul(h1, w_q, epi="scale", scale=MOBA_HD ** -0.5, out_dtype=BF16, name=f"q_proj_{tag}")
    o = moba_fn(q, k_sh, v_sh)
    x = _matmul(o, w_o, epi="res", res=x, name=f"moba_out_{tag}")
    x = _mlp(x, norm_mlp[1], w_up[1], w_down[1], f"1_{tag}")

    (y,) = _rmsnorm(x, norm_final[None, :], F32, name=f"rms_final_{tag}")
    return y, s_new, k_sh, v_sh


def kernel(x_prompt, x_sample, state_gla, cache_k, cache_v, page_table, rel_bias_table, norm_mix, norm_mlp, norm_kv, norm_final, w_gla_in, w_gla_gate_up, b_gla_gate, gla_head_norm, w_gla_out, w_kv, w_q_moba, w_o_moba, w_mlp_up, w_mlp_down):
    bp, tp, d = x_prompt.shape
    bs, ts, _ = x_sample.shape
    n_a = w_gla_in.shape[0]
    assert n_a == 1 and norm_mix.shape[0] == 2, "one GLA layer followed by one MoBA layer"
    nh = w_q_moba.shape[2] // MOBA_HD
    n_pages = page_table.shape[1]
    past = n_pages * PAGE_SIZE
    assert past % MOBA_BLOCK == 0 and past // MOBA_BLOCK >= MOBA_TOPK
    nfp = past // MOBA_BLOCK

    qkvr = w_gla_in.shape[2] - GLA_GATE_RANK
    bf = lambda w: w.astype(BF16)
    w_low = jnp.pad(bf(w_gla_in[0][:, qkvr:]), ((0, 0), (0, V7X_LANES - GLA_GATE_RANK)))
    w_gate_up = jnp.pad(bf(w_gla_gate_up[0]), ((0, V7X_LANES - GLA_GATE_RANK), (0, 0)))
    wts = (norm_mix, norm_mlp, norm_kv, norm_final, bf(w_gla_in[0]), w_low, w_gate_up, b_gla_gate[0][None, :],
           gla_head_norm[0][None, :], bf(w_gla_out[0]), bf(w_kv), bf(w_q_moba[0]), bf(w_o_moba[0]),
           bf(w_mlp_up), bf(w_mlp_down))

    def moba_prompt_fn(q, k, v):
        return _moba_prompt(q, k, v, rel_bias_table, batch=bp, seq=tp)

    def moba_sample_fn(q, k, v):
        rows = V7X_BF16_SUBLANES
        width = q.shape[1]
        pad3 = lambda a: jnp.pad(a.reshape(bs, ts, width), ((0, 0), (0, rows - ts), (0, 0)))
        q_pad = pad3(q)
        sel = _moba_select(q_pad, cache_k, page_table, nfp=nfp, nq=ts)[:, :, :MOBA_TOPK]
        o = _moba_sample(q_pad, pad3(k), pad3(v), cache_k, cache_v, sel.reshape(-1), page_table.reshape(-1),
                         rel_bias_table, nq=ts, past=past)
        return o[:, :ts].reshape(bs * ts, width).astype(BF16)

    y_p, s_p, k_p, v_p = _trunk(x_prompt.reshape(bp * tp, d), None, moba_prompt_fn, wts, batch=bp, seq=tp, tag="p")
    y_s, s_s, k_s, v_s = _trunk(x_sample.reshape(bs * ts, d), state_gla[0], moba_sample_fn, wts, batch=bs, seq=ts, tag="s")
    hd4 = lambda a, b, t: a.reshape(b, t, nh, MOBA_HD)
    return (y_p.reshape(bp, tp, d), y_s.reshape(bs, ts, d), s_p[None], s_s[None],
            hd4(k_p, bp, tp), hd4(v_p, bp, tp), hd4(k_s, bs, ts), hd4(v_s, bs, ts))
```

```python
import functools
import math

import jax
import jax.numpy as jnp
from jax import lax
from jax.experimental import pallas as pl
from jax.experimental.pallas import tpu as pltpu

F32 = jnp.float32
BF16 = jnp.bfloat16

EPS = 1e-6
GLA_HEADS = 4
GLA_GATE_RANK = 16
GLA_GATE_TAU = 16.0
GLA_CHUNK = 64
MOBA_HD = 128
MOBA_BLOCK = 256
MOBA_TOPK = 3
PAGE_SIZE = 128
REL_BUCKETS = 32
REL_MAX_DIST = 1024
NEG_INF = -1e30

V7X_LANES = 128
V7X_BF16_SUBLANES = 16
V7X_VMEM_BYTES = 64 * 1024 * 1024
V7X_VMEM_LIMIT_BYTES = V7X_VMEM_BYTES - 8 * 1024 * 1024
MM_VMEM_BUDGET_BYTES = 40 * 1024 * 1024


def _cparams(semantics):
    return pltpu.CompilerParams(dimension_semantics=semantics, vmem_limit_bytes=V7X_VMEM_LIMIT_BYTES)


def _dot_nt(a, b):
    return lax.dot_general(a, b, (((1,), (1,)), ((), ())), preferred_element_type=F32)


def _dot_tn(a, b):
    return lax.dot_general(a, b, (((0,), (0,)), ((), ())), preferred_element_type=F32)


def _dot(a, b):
    return jnp.dot(a, b, preferred_element_type=F32)


def _div_pow2(x, d):
    assert d & (d - 1) == 0
    return lax.shift_right_logical(x, d.bit_length() - 1)


def _t5_bucket(n):
    max_exact = REL_BUCKETS // 2
    if n < max_exact:
        return n
    large = max_exact + int(math.log(n / max_exact) / math.log(REL_MAX_DIST / max_exact) * (REL_BUCKETS - max_exact))
    return min(large, REL_BUCKETS - 1)


def _t5_thresholds():
    thr, n = [], 0
    for m in range(REL_BUCKETS):
        while _t5_bucket(n) < m:
            n += 1
        thr.append(n)
    return thr


def _bias_lookup(dist, tab, lo=None, hi=None):
    thr = _t5_thresholds()
    lo = 0 if lo is None else max(lo, 0)
    base = _t5_bucket(lo)
    val = jnp.full(dist.shape, tab(base), F32)
    for m in range(base + 1, REL_BUCKETS):
        if hi is not None and thr[m] > hi:
            break
        val = jnp.where(dist >= thr[m], tab(m), val)
    return val


def _side_cast(side_in, side_out):
    for src, dst in zip(side_in, side_out):
        dst[...] = src[...].astype(BF16)


def _side_cast_specs(side, grid, step_of):
    nsteps = math.prod(grid)
    in_specs, out_specs, shapes, args = [], [], [], []
    for item in side:
        arr, layer = item if isinstance(item, tuple) else (item, None)
        rows, cols = arr.shape[-2:]
        rb = next(r for r in range(V7X_BF16_SUBLANES, rows + 1, V7X_BF16_SUBLANES)
                  if rows % r == 0 and rows // r <= nsteps)
        nb = rows // rb

        def blk(*idx, nb=nb):
            return jnp.minimum(step_of(*idx), nb - 1)

        if layer is None:
            in_specs.append(pl.BlockSpec((rb, cols), lambda *idx, blk=blk: (blk(*idx), 0)))
        else:
            in_specs.append(pl.BlockSpec((None, rb, cols), lambda *idx, blk=blk, layer=layer: (layer, blk(*idx), 0)))
        out_specs.append(pl.BlockSpec((rb, cols), lambda *idx, blk=blk: (blk(*idx), 0)))
        shapes.append(jax.ShapeDtypeStruct((rows, cols), BF16))
        args.append(arr)
    return in_specs, out_specs, shapes, args


def _mm_body(*refs, nk, epi, scale, n_side):
    n_in = (3 if epi == "res" else 2) + n_side
    x_ref, w_ref = refs[:2]
    r_ref = refs[2] if epi == "res" else None
    side_in = refs[n_in - n_side:n_in]
    o_ref = refs[n_in]
    side_out = refs[n_in + 1:n_in + 1 + n_side]
    scratch = refs[n_in + 1 + n_side:]
    _side_cast(side_in, side_out)

    def finish(acc):
        if epi == "res":
            o_ref[...] = r_ref[...] + acc
        elif epi == "relu2":
            a = jnp.maximum(acc, 0.0)
            o_ref[...] = (a * a).astype(o_ref.dtype)
        elif epi == "scale":
            o_ref[...] = (acc * scale).astype(o_ref.dtype)
        else:
            o_ref[...] = acc.astype(o_ref.dtype)

    part = _dot(x_ref[...], w_ref[...])
    if nk == 1:
        finish(part)
    else:
        acc_ref = scratch[0]
        k = pl.program_id(2)

        @pl.when(k == 0)
        def _():
            acc_ref[...] = part

        @pl.when(k > 0)
        def _():
            acc_ref[...] += part

        @pl.when(k == nk - 1)
        def _():
            finish(acc_ref[...])


def _mm_tiles(m, k, n, out_bytes, has_res):
    tm = min(m, 1024)
    tk = min(k, 4096)
    for tn in (1024, 512, 256, 128):
        if n % tn:
            continue
        windows = tm * tk * 2 + tk * tn * 2 + tm * tn * out_bytes + (tm * tn * 4 if has_res else 0)
        acc = tm * tn * 4 if tk != k else 0
        if 2 * windows + acc <= MM_VMEM_BUDGET_BYTES:
            return tm, tn, tk
    raise ValueError(f"no matmul tiling fits VMEM for {(m, k, n)}")


def _matmul(x, w, *, n0=0, n=None, epi="plain", out_dtype=F32, res=None, scale=1.0, side=(), name="mm"):
    m, k = x.shape
    n = w.shape[1] if n is None else n
    assert x.dtype == BF16 and w.dtype == BF16 and w.shape[0] == k
    assert (res is not None) == (epi == "res")
    tm, tn, tk = _mm_tiles(m, k, n, jnp.dtype(out_dtype).itemsize, res is not None)
    assert m % tm == 0 and n % tn == 0 and k % tk == 0 and n0 % tn == 0
    nk = k // tk
    j0 = n0 // tn
    grid = (m // tm, n // tn, nk)
    in_specs = [
        pl.BlockSpec((tm, tk), lambda i, j, kk: (i, kk)),
        pl.BlockSpec((tk, tn), lambda i, j, kk: (kk, j + j0)),
    ]
    args = [x, w]
    if res is not None:
        in_specs.append(pl.BlockSpec((tm, tn), lambda i, j, kk: (i, j)))
        args.append(res)
    side_in, side_out, side_shapes, side_args = _side_cast_specs(
        side, grid, lambda i, j, kk: (i * grid[1] + j) * grid[2] + kk)
    outs = pl.pallas_call(
        functools.partial(_mm_body, nk=nk, epi=epi, scale=scale, n_side=len(side)),
        out_shape=[jax.ShapeDtypeStruct((m, n), out_dtype)] + side_shapes,
        grid=grid,
        in_specs=in_specs + side_in,
        out_specs=[pl.BlockSpec((tm, tn), lambda i, j, kk: (i, j))] + side_out,
        scratch_shapes=[pltpu.VMEM((tm, tn), F32)] if nk > 1 else [],
        compiler_params=_cparams(("arbitrary",) * 3 if side else ("parallel", "parallel", "arbitrary")),
        name=name,
    )(*args, *side_args)
    return outs if side else outs[0]


def _rms_body(x_ref, g_ref, *o_refs):
    x = x_ref[...]
    y = x * lax.rsqrt(jnp.mean(x * x, axis=-1, keepdims=True) + EPS)
    for gi, o_ref in enumerate(o_refs):
        o_ref[...] = (y * g_ref[gi:gi + 1, :]).astype(o_ref.dtype)


def _rmsnorm(x, gains, out_dtype, name="rmsnorm"):
    m, d = x.shape
    g = gains.shape[0]
    tm = min(m, 256)
    assert m % tm == 0
    outs = pl.pallas_call(
        _rms_body,
        out_shape=[jax.ShapeDtypeStruct((m, d), out_dtype)] * g,
        grid=(m // tm,),
        in_specs=[pl.BlockSpec((tm, d), lambda i: (i, 0)), pl.BlockSpec((g, d), lambda i: (0, 0))],
        out_specs=[pl.BlockSpec((tm, d), lambda i: (i, 0))] * g,
        compiler_params=_cparams(("parallel",)),
        name=name,
    )(x, gains)
    return outs


def _gate_body(h_ref, wl_ref, wu_ref, b_ref, o_ref):
    g_low = _dot(h_ref[...], wl_ref[...])
    z = _dot(g_low.astype(BF16), wu_ref[...]) + b_ref[...]
    log_sig = jnp.minimum(z, 0.0) - jnp.log1p(jnp.exp(-jnp.abs(z)))
    o_ref[...] = log_sig / GLA_GATE_TAU


def _gla_gate(h, w_low, w_up, b):
    m, d = h.shape
    qk = w_up.shape[1]
    tm = min(m, 512)
    assert m % tm == 0
    return pl.pallas_call(
        _gate_body,
        out_shape=jax.ShapeDtypeStruct((m, qk), F32),
        grid=(m // tm,),
        in_specs=[
            pl.BlockSpec((tm, d), lambda i: (i, 0)),
            pl.BlockSpec(w_low.shape, lambda i: (0, 0)),
            pl.BlockSpec(w_up.shape, lambda i: (0, 0)),
            pl.BlockSpec((1, qk), lambda i: (0, 0)),
        ],
        out_specs=pl.BlockSpec((tm, qk), lambda i: (i, 0)),
        compiler_params=_cparams(("parallel",)),
        name="gla_gate",
    )(h, w_low, w_up, b)


def _gla_body(*refs, chunk, dk, dv, has_s0, n_side):
    q_ref, k_ref, v_ref, r_ref, g_ref, hn_ref = refs[:6]
    n_in = 6 + (1 if has_s0 else 0) + n_side
    s0_ref = refs[6] if has_s0 else None
    side_in = refs[n_in - n_side:n_in]
    o_ref, so_ref = refs[n_in:n_in + 2]
    side_out = refs[n_in + 2:n_in + 2 + n_side]
    s_ref = refs[n_in + 2 + n_side]
    _side_cast(side_in, side_out)
    c = pl.program_id(2)

    @pl.when(c == 0)
    def _():
        s_ref[...] = s0_ref[...] if has_s0 else jnp.zeros_like(s_ref)

    q = q_ref[...] * dk ** -0.5
    k = k_ref[...]
    v = v_ref[...].astype(BF16)
    b = g_ref[...]
    rows = lax.broadcasted_iota(jnp.int32, b.shape, 0)
    shift = 1
    while shift < chunk:
        b = b + jnp.where(rows >= shift, pltpu.roll(b, shift, 0), 0.0)
        shift *= 2
    q_in = (q * jnp.exp(b)).astype(BF16)
    k_in = (k * jnp.exp(-b)).astype(BF16)
    att = _dot_nt(q_in, k_in)
    tri = lax.broadcasted_iota(jnp.int32, att.shape, 0) >= lax.broadcasted_iota(jnp.int32, att.shape, 1)
    att = jnp.where(tri, att, 0.0)
    s = s_ref[...]
    o = _dot(q_in, s.astype(BF16)) + _dot(att.astype(BF16), v)
    b_last = b[chunk - 1:chunk, :]
    k_dec = (k * jnp.exp(b_last - b)).astype(BF16)
    decay_rows = jnp.transpose(jnp.broadcast_to(jnp.exp(b_last), (V7X_LANES, dk)))
    decay = jnp.concatenate([decay_rows] * (dv // V7X_LANES), axis=1)
    s_new = decay * s + _dot_tn(k_dec, v)
    s_ref[...] = s_new

    y = o * lax.rsqrt(jnp.mean(o * o, axis=-1, keepdims=True) + EPS) * hn_ref[...]
    r = r_ref[...]
    o_ref[...] = (y * (r * jax.nn.sigmoid(r))).astype(o_ref.dtype)

    @pl.when(c == pl.num_programs(2) - 1)
    def _():
        so_ref[...] = s_new


def _gla(proj, gk, head_norm, s0, *, batch, seq, chunk, out_dtype, side=()):
    h = GLA_HEADS
    dk = gk.shape[1] // h
    dv = (proj.shape[1] - 2 * h * dk) // (2 * h)
    assert seq % chunk == 0 and (h * dk) % dv == 0
    nc = seq // chunk
    v_blk0 = (2 * h * dk) // dv
    r_blk0 = v_blk0 + h
    row = lambda b, hh, c: b * nc + c
    in_specs = [
        pl.BlockSpec((chunk, dk), lambda b, hh, c: (row(b, hh, c), hh)),
        pl.BlockSpec((chunk, dk), lambda b, hh, c: (row(b, hh, c), h + hh)),
        pl.BlockSpec((chunk, dv), lambda b, hh, c: (row(b, hh, c), v_blk0 + hh)),
        pl.BlockSpec((chunk, dv), lambda b, hh, c: (row(b, hh, c), r_blk0 + hh)),
        pl.BlockSpec((chunk, dk), lambda b, hh, c: (row(b, hh, c), hh)),
        pl.BlockSpec((1, dv), lambda b, hh, c: (0, 0)),
    ]
    args = [proj, proj, proj, proj, gk, head_norm]
    if s0 is not None:
        in_specs.append(pl.BlockSpec((None, None, dk, dv), lambda b, hh, c: (b, hh, 0, 0)))
        args.append(s0)
    grid = (batch, h, nc)
    side_in, side_out, side_shapes, side_args = _side_cast_specs(
        side, grid, lambda b, hh, c: (b * h + hh) * nc + c)
    return pl.pallas_call(
        functools.partial(_gla_body, chunk=chunk, dk=dk, dv=dv, has_s0=s0 is not None, n_side=len(side)),
        out_shape=[
            jax.ShapeDtypeStruct((batch * seq, h * dv), out_dtype),
            jax.ShapeDtypeStruct((batch, h, dk, dv), F32),
        ] + side_shapes,
        grid=grid,
        in_specs=in_specs + side_in,
        out_specs=[
            pl.BlockSpec((chunk, dv), lambda b, hh, c: (row(b, hh, c), hh)),
            pl.BlockSpec((None, None, dk, dv), lambda b, hh, c: (b, hh, 0, 0)),
        ] + side_out,
        scratch_shapes=[pltpu.VMEM((dk, dv), F32)],
        compiler_params=_cparams(("arbitrary",) * 3 if side else ("parallel", "parallel", "arbitrary")),
        name="gla_recurrence",
    )(*args, *side_args)


def _moba_prompt_body(tab_ref, q_ref, k_ref, v_ref, o_ref, kb_ref, vt_ref, bias_ref, *, nblk):
    blk = MOBA_BLOCK
    hh = pl.program_id(0)
    b = pl.program_id(1)
    tab = lambda m: tab_ref[m, hh]

    @pl.when(b == 0)
    def _():
        ki = lax.broadcasted_iota(jnp.int32, (blk, blk), 0)
        qj = lax.broadcasted_iota(jnp.int32, (blk, blk), 1)
        for d in range(nblk):
            dist = d * blk + qj - ki
            bias_ref[d] = _bias_lookup(dist, tab, lo=d * blk - (blk - 1), hi=d * blk + (blk - 1))

    k = k_ref[...]
    t = k.shape[0]
    kb_ref[...] = k.astype(BF16)
    v = v_ref[...]
    for n in range(nblk):
        vt_ref[:, n * blk:(n + 1) * blk] = jnp.transpose(v[n * blk:(n + 1) * blk, :]).astype(BF16)

    kmean = jnp.sum(k.reshape(nblk, blk, k.shape[1]), axis=1) * (1.0 / blk)
    q_all = q_ref[...]
    gate = _dot_nt(kmean.astype(BF16), q_all)
    n_iota = lax.broadcasted_iota(jnp.int32, (nblk, t), 0)
    own = _div_pow2(lax.broadcasted_iota(jnp.int32, (nblk, t), 1), blk)
    gate = jnp.where(n_iota < own, gate, NEG_INF)
    rank = jnp.zeros((nblk, t), F32)
    for m in range(nblk):
        gm = jnp.broadcast_to(gate[m:m + 1, :], (nblk, t))
        tie_before = jnp.where(n_iota > m, 1.0, 0.0)
        rank = rank + jnp.where(gm > gate, 1.0, jnp.where(gm == gate, tie_before, 0.0))
    sel = jnp.where(rank < MOBA_TOPK, jnp.where(n_iota < own, 1.0, 0.0), 0.0)

    ki = lax.broadcasted_iota(jnp.int32, (blk, blk), 0)
    qj = lax.broadcasted_iota(jnp.int32, (blk, blk), 1)
    causal = ki <= qj

    for qi in range(nblk):
        nk = (qi + 1) * blk
        q_t = q_ref[qi * blk:nk, :]
        s = _dot_nt(kb_ref[0:nk, :], q_t)
        parts = []
        for j in range(qi + 1):
            sj = s[j * blk:(j + 1) * blk, :] + bias_ref[qi - j]
            if j == qi:
                sj = jnp.where(causal, sj, NEG_INF)
            else:
                sj = jnp.where(sel[j:j + 1, qi * blk:nk] > 0.0, sj, NEG_INF)
            parts.append(sj)
        m = jnp.max(functools.reduce(jnp.maximum, parts), axis=0, keepdims=True)
        probs = [jnp.exp(sj - m) for sj in parts]
        l = jnp.sum(functools.reduce(jnp.add, probs), axis=0, keepdims=True)
        p_all = jnp.concatenate([p.astype(BF16) for p in probs], axis=0)
        acc = _dot(vt_ref[:, 0:nk], p_all)
        o_ref[qi * blk:nk, :] = jnp.transpose(acc / l).astype(o_ref.dtype)


def _moba_prompt(q, k, v, table, *, batch, seq):
    hd = MOBA_HD
    nh = q.shape[1] // hd
    assert seq % MOBA_BLOCK == 0
    nblk = seq // MOBA_BLOCK
    spec = pl.BlockSpec((seq, hd), lambda hh, b: (b, hh))
    return pl.pallas_call(
        functools.partial(_moba_prompt_body, nblk=nblk),
        out_shape=jax.ShapeDtypeStruct(q.shape, BF16),
        grid=(nh, batch),
        in_specs=[pl.BlockSpec(memory_space=pltpu.SMEM), spec, spec, spec],
        out_specs=spec,
        scratch_shapes=[
            pltpu.VMEM((seq, hd), BF16),
            pltpu.VMEM((hd, seq), BF16),
            pltpu.VMEM((nblk, MOBA_BLOCK, MOBA_BLOCK), F32),
        ],
        compiler_params=_cparams(("arbitrary", "arbitrary")),
        name="moba_prompt",
    )(table, q, k, v)


def _moba_select_body(pt_ref, ka_ref, kb_ref, q_ref, o_ref, kmt_ref, *, nfp, nh, nq):
    del pt_ref
    n = pl.program_id(1)
    hd = MOBA_HD
    ps = (jnp.sum(ka_ref[...], axis=0) + jnp.sum(kb_ref[...], axis=0)) * (1.0 / MOBA_BLOCK)
    kmt_ref[pl.ds(pl.multiple_of(n * nh, nh), nh), :] = ps

    @pl.when(n == nfp - 1)
    def _():
        q = q_ref[...]
        out_lane = lax.broadcasted_iota(jnp.int32, (q.shape[0], V7X_LANES), 1)
        for hh in range(nh):
            km = kmt_ref[pl.ds(hh, nfp, stride=nh), :]
            gate = _dot_nt(q[:, hh * hd:(hh + 1) * hd], km.astype(BF16))
            lane = lax.broadcasted_iota(jnp.int32, gate.shape, 1).astype(F32)
            res = jnp.zeros(out_lane.shape, F32)
            for t in range(MOBA_TOPK):
                m = jnp.max(gate, axis=1, keepdims=True)
                pick = jnp.min(jnp.where(gate == m, lane, float(nfp)), axis=1, keepdims=True)
                res = jnp.where(out_lane == t, pick, res)
                gate = jnp.where(lane == pick, -jnp.inf, gate)
            o_ref[hh * nq:(hh + 1) * nq, :] = res[:nq].astype(jnp.int32)


def _moba_select(q, cache_k, page_table, *, nfp, nq):
    bsz, rows, width = q.shape
    _, page, nh, hd = cache_k.shape
    ppb = MOBA_BLOCK // page
    assert ppb == 2 and page == PAGE_SIZE
    page_spec = lambda off: pl.BlockSpec((None, page, nh, hd), lambda b, n, pt: (pt[b, ppb * n + off], 0, 0, 0))
    return pl.pallas_call(
        functools.partial(_moba_select_body, nfp=nfp, nh=nh, nq=nq),
        out_shape=jax.ShapeDtypeStruct((bsz, nh * nq, V7X_LANES), jnp.int32),
        grid_spec=pltpu.PrefetchScalarGridSpec(
            num_scalar_prefetch=1,
            grid=(bsz, nfp),
            in_specs=[page_spec(0), page_spec(1), pl.BlockSpec((None, rows, width), lambda b, n, pt: (b, 0, 0))],
            out_specs=pl.BlockSpec((None, nh * nq, V7X_LANES), lambda b, n, pt: (b, 0, 0)),
            scratch_shapes=[pltpu.VMEM((nfp * nh, hd), F32)],
        ),
        compiler_params=_cparams(("parallel", "arbitrary")),
        name="moba_select",
    )(page_table, cache_k, cache_k, q)


def _moba_sample_body(sel_ref, pt_ref, tab_ref, q_ref, kn_ref, vn_ref, ck_ref, cv_ref, o_ref,
                      kbuf, vbuf, knew, vnew, ksem, vsem, *, nh, nq, n_pages, past):
    t = pl.program_id(0)
    nsteps = pl.num_programs(0)
    blk = MOBA_BLOCK
    ppb = blk // PAGE_SIZE
    nsel = nq * MOBA_TOPK
    slot = t % 2

    def copies(step, slot_):
        bb = step // nh
        hh = step % nh
        out = []
        for c in range(nsel):
            block_id = sel_ref[step * nsel + c]
            for pg in range(ppb):
                page = pt_ref[bb * n_pages + block_id * ppb + pg]
                dst = pl.ds((c * ppb + pg) * PAGE_SIZE, PAGE_SIZE)
                out.append(pltpu.make_async_copy(ck_ref.at[page, :, hh, :], kbuf.at[slot_, dst, :], ksem.at[slot_]))
                out.append(pltpu.make_async_copy(cv_ref.at[page, :, hh, :], vbuf.at[slot_, dst, :], vsem.at[slot_]))
        return out

    @pl.when(t == 0)
    def _():
        for cp in copies(t, slot):
            cp.start()

    @pl.when(t + 1 < nsteps)
    def _():
        for cp in copies(t + 1, 1 - slot):
            cp.start()

    hh = t % nh
    tab = lambda m: tab_ref[m, hh]
    q = q_ref[...]
    rows = q.shape[0]
    width = nsel * blk

    knew[...] = jnp.zeros_like(knew)
    vnew[...] = jnp.zeros_like(vnew)
    knew[pl.ds(0, rows), :] = kn_ref[...]
    vnew[pl.ds(0, rows), :] = vn_ref[...]
    lo = _dot_nt(q, knew[...].astype(BF16))
    r_o = lax.broadcasted_iota(jnp.int32, lo.shape, 0)
    t_o = lax.broadcasted_iota(jnp.int32, lo.shape, 1)
    d_o = r_o - t_o
    lo = lo + _bias_lookup(d_o, tab, lo=0, hi=nq - 1)
    lo = jnp.where(d_o >= 0, jnp.where(t_o < nq, lo, NEG_INF), NEG_INF)

    lane = lax.broadcasted_iota(jnp.int32, (rows, width), 1)
    row = lax.broadcasted_iota(jnp.int32, (rows, width), 0)
    kstart = jnp.zeros((rows, width), jnp.int32)
    for c in range(nsel):
        kstart = jnp.where(_div_pow2(lane, blk) == c, sel_ref[t * nsel + c] * blk, kstart)
    dist = past + row - (kstart + (lane & (blk - 1)))

    for cp in copies(t, slot):
        cp.wait()

    ls = _dot_nt(q, kbuf[slot].astype(BF16)) + _bias_lookup(dist, tab)
    first = row * (MOBA_TOPK * blk)
    ls = jnp.where(lane >= first, jnp.where(lane < first + MOBA_TOPK * blk, ls, NEG_INF), NEG_INF)

    m = jnp.maximum(jnp.max(lo, axis=1, keepdims=True), jnp.max(ls, axis=1, keepdims=True))
    p_o = jnp.exp(lo - m)
    p_s = jnp.exp(ls - m)
    den = jnp.sum(p_o, axis=1, keepdims=True) + jnp.sum(p_s, axis=1, keepdims=True)
    out = _dot(p_o.astype(BF16), vnew[...].astype(BF16)) + _dot(p_s.astype(BF16), vbuf[slot].astype(BF16))
    o_ref[...] = out / den


def _moba_sample(q, k_new, v_new, cache_k, cache_v, sel, page_table, table, *, nq, past):
    bsz, rows, width = q.shape
    hd = MOBA_HD
    nh = width // hd
    n_pages = page_table.shape[0] // bsz
    nsel = nq * MOBA_TOPK
    tok = pl.BlockSpec((None, rows, hd), lambda t, sel, pt: (t // nh, 0, t % nh))
    return pl.pallas_call(
        functools.partial(_moba_sample_body, nh=nh, nq=nq, n_pages=n_pages, past=past),
        out_shape=jax.ShapeDtypeStruct((bsz, rows, width), F32),
        grid_spec=pltpu.PrefetchScalarGridSpec(
            num_scalar_prefetch=2,
            grid=(bsz * nh,),
            in_specs=[
                pl.BlockSpec(memory_space=pltpu.SMEM),
                tok, tok, tok,
                pl.BlockSpec(memory_space=pl.ANY),
                pl.BlockSpec(memory_space=pl.ANY),
            ],
            out_specs=tok,
            scratch_shapes=[
                pltpu.VMEM((2, nsel * MOBA_BLOCK, hd), F32),
                pltpu.VMEM((2, nsel * MOBA_BLOCK, hd), F32),
                pltpu.VMEM((V7X_LANES, hd), F32),
                pltpu.VMEM((V7X_LANES, hd), F32),
                pltpu.SemaphoreType.DMA((2,)),
                pltpu.SemaphoreType.DMA((2,)),
            ],
        ),
        compiler_params=_cparams(("arbitrary",)),
        name="moba_sample",
    )(sel, page_table, table, q, k_new, v_new, cache_k, cache_v)


def _pad_rows(a, batch, seq, rows):
    w = a.shape[1]
    return jnp.pad(a.reshape(batch, seq, w), ((0, 0), (0, rows - seq), (0, 0))).reshape(batch * rows, w)


def _trunk(x, s0, moba_fn, p, wb, *, batch, seq, tag):
    def run(fn, *args, casts=(), **kw):
        need = [name for name in casts if name not in wb]
        outs = fn(*args, side=[p[name] for name in need], **kw)
        if not need:
            return outs
        main, conv = outs[:len(outs) - len(need)], outs[len(outs) - len(need):]
        wb.update(zip(need, conv))
        return main[0] if len(main) == 1 else main

    kvw = p["w_kv"].shape[1] // 2
    qkvr = wb["w_in"].shape[1] - GLA_GATE_RANK

    (h0,) = _rmsnorm(x, p["norm_mix"][0:1], BF16, name=f"rms_mix0_{tag}")
    proj = run(_matmul, h0, wb["w_in"], n=qkvr, casts=("w_out",), name=f"gla_in_{tag}")
    gk = _gla_gate(h0, wb["w_low"], wb["w_gate_up"], p["b_gate"])
    if seq % GLA_CHUNK == 0:
        og, s_new = run(_gla, proj, gk, p["head_norm"], s0, casts=("w_up0",),
                        batch=batch, seq=seq, chunk=GLA_CHUNK, out_dtype=BF16)
    else:
        rows = V7X_BF16_SUBLANES
        assert seq <= rows
        og, s_new = run(_gla, _pad_rows(proj, batch, seq, rows), _pad_rows(gk, batch, seq, rows), p["head_norm"], s0,
                        casts=("w_up0",), batch=batch, seq=rows, chunk=rows, out_dtype=BF16)
        og = og.reshape(batch, rows, -1)[:, :seq].reshape(batch * seq, -1)
    x = _matmul(og, wb["w_out"], epi="res", res=x, name=f"gla_out_{tag}")
    (h,) = _rmsnorm(x, p["norm_mlp"][0:1], BF16, name=f"rms_mlp0_{tag}")
    u = run(_matmul, h, wb["w_up0"], epi="relu2", out_dtype=BF16, casts=("w_down0",), name=f"mlp_up0_{tag}")
    x = run(_matmul, u, wb["w_down0"], epi="res", res=x, casts=("w_up1", "w_kv"), name=f"mlp_down0_{tag}")

    h1, kvn = _rmsnorm(x, jnp.stack([p["norm_mix"][1], p["norm_kv"]]), BF16, name=f"rms_mix1_{tag}")
    k_sh = run(_matmul, kvn, wb["w_kv"], n0=0, n=kvw, casts=("w_q",), name=f"k_proj_{tag}")
    v_sh = run(_matmul, kvn, wb["w_kv"], n0=kvw, n=kvw, casts=("w_o",), name=f"v_proj_{tag}")
    q = _matmul(h1, wb["w_q"], epi="scale", scale=MOBA_HD ** -0.5, out_dtype=BF16, name=f"q_proj_{tag}")
    o = moba_fn(q, k_sh, v_sh)
    x = _matmul(o, wb["w_o"], epi="res", res=x, name=f"moba_out_{tag}")
    (h,) = _rmsnorm(x, p["norm_mlp"][1:2], BF16, name=f"rms_mlp1_{tag}")
    u = run(_matmul, h, wb["w_up1"], epi="relu2", out_dtype=BF16, casts=("w_down1",), name=f"mlp_up1_{tag}")
    x = _matmul(u, wb["w_down1"], epi="res", res=x, name=f"mlp_down1_{tag}")

    (y,) = _rmsnorm(x, p["norm_final"][None, :], F32, name=f"rms_final_{tag}")
    return y, s_new, k_sh, v_sh


def kernel(x_prompt, x_sample, state_gla, cache_k, cache_v, page_table, rel_bias_table, norm_mix, norm_mlp, norm_kv, norm_final, w_gla_in, w_gla_gate_up, b_gla_gate, gla_head_norm, w_gla_out, w_kv, w_q_moba, w_o_moba, w_mlp_up, w_mlp_down):
    bp, tp, d = x_prompt.shape
    bs, ts, _ = x_sample.shape
    n_a = w_gla_in.shape[0]
    assert n_a == 1 and norm_mix.shape[0] == 2, "one GLA layer followed by one MoBA layer"
    nh = w_q_moba.shape[2] // MOBA_HD
    n_pages = page_table.shape[1]
    past = n_pages * PAGE_SIZE
    assert past % MOBA_BLOCK == 0 and past // MOBA_BLOCK >= MOBA_TOPK
    nfp = past // MOBA_BLOCK

    p = dict(norm_mix=norm_mix, norm_mlp=norm_mlp, norm_kv=norm_kv, norm_final=norm_final,
             b_gate=b_gla_gate[0][None, :], head_norm=gla_head_norm[0][None, :],
             w_out=(w_gla_out, 0), w_up0=(w_mlp_up, 0), w_down0=(w_mlp_down, 0), w_up1=(w_mlp_up, 1),
             w_down1=(w_mlp_down, 1), w_kv=w_kv, w_q=(w_q_moba, 0), w_o=(w_o_moba, 0))
    qkvr = w_gla_in.shape[2] - GLA_GATE_RANK
    w_in = w_gla_in[0].astype(BF16)
    wb = dict(w_in=w_in,
              w_low=jnp.pad(w_in[:, qkvr:], ((0, 0), (0, V7X_LANES - GLA_GATE_RANK))),
              w_gate_up=jnp.pad(w_gla_gate_up[0].astype(BF16), ((0, V7X_LANES - GLA_GATE_RANK), (0, 0))))

    def moba_prompt_fn(q, k, v):
        return _moba_prompt(q, k, v, rel_bias_table, batch=bp, seq=tp)

    def moba_sample_fn(q, k, v):
        rows = V7X_BF16_SUBLANES
        width = q.shape[1]
        pad3 = lambda a: jnp.pad(a.reshape(bs, ts, width), ((0, 0), (0, rows - ts), (0, 0)))
        q_pad = pad3(q)
        sel = _moba_select(q_pad, cache_k, page_table, nfp=nfp, nq=ts)[:, :, :MOBA_TOPK]
        o = _moba_sample(q_pad, pad3(k), pad3(v), cache_k, cache_v, sel.reshape(-1), page_table.reshape(-1),
                         rel_bias_table, nq=ts, past=past)
        return o[:, :ts].reshape(bs * ts, width).astype(BF16)

    y_p, s_p, k_p, v_p = _trunk(x_prompt.reshape(bp * tp, d), None, moba_prompt_fn, p, wb, batch=bp, seq=tp, tag="p")
    y_s, s_s, k_s, v_s = _trunk(x_sample.reshape(bs * ts, d), state_gla[0], moba_sample_fn, p, wb, batch=bs, seq=ts, tag="s")
    hd4 = lambda a, b, t: a.reshape(b, t, nh, MOBA_HD)
    return (y_p.reshape(bp, tp, d), y_s.reshape(bs, ts, d), s_p[None], s_s[None],
            hd4(k_p, bp, tp), hd4(v_p, bp, tp), hd4(k_s, bs, ts), hd4(v_s, bs, ts))
```
